```python
import math
import jax
import jax.numpy as jnp
from jax import lax
import numpy as np

D_MODEL = 1024
BATCH = 4
SEQ = 8192
DEPTH = 2

CTX_LEN = 256
GRID_W = 64
CHUNK = 128
EPS = 1e-6

SSD_INNER = 2 * D_MODEL
SSD_HEAD_DIM = 64
SSD_HEADS = SSD_INNER // SSD_HEAD_DIM
SSD_GROUPS = 4
SSD_STATE = 128
SSD_CONV = 3
XBC_DIM = SSD_INNER + 2 * SSD_GROUPS * SSD_STATE

RET_HEADS = 8
RET_QK_DIM = D_MODEL // RET_HEADS
RET_V_DIM = 2 * RET_QK_DIM
RET_QK_WIDTH = RET_HEADS * RET_QK_DIM
RET_V_WIDTH = RET_HEADS * RET_V_DIM
ROPE_BASE = 10000.0

N_EXPERTS = 16
EXPERT_FF = 2 * D_MODEL
CAPACITY_FACTOR = 2

PROJ_SIZES = (SSD_INNER, XBC_DIM, SSD_HEADS, RET_QK_WIDTH, RET_QK_WIDTH, RET_V_WIDTH, RET_V_WIDTH, D_MODEL, D_MODEL)
PROJ_DIM = SSD_INNER + XBC_DIM + SSD_HEADS + 2 * RET_QK_WIDTH + 2 * RET_V_WIDTH + 2 * D_MODEL

kernel_name = 'hybrid_ssd_retention_ecmoe_dit'

F32 = jnp.float32


def rmsnorm(x, w):
    xf = x.astype(F32)
    y = xf * lax.rsqrt(jnp.mean(xf * xf, axis=-1, keepdims=True) + EPS)
    return (y * w.astype(F32)).astype(x.dtype)


def modulate(h, shift, scale):
    return h * (1 + scale) + shift


def split_cols(p, sizes):
    out, start = [], 0
    for s in sizes:
        out.append(p[..., start:start + s])
        start += s
    return out


def centred_dwconv(x, w, b):
    y = lax.conv_general_dilated(
        x, w[:, None, :].astype(x.dtype), window_strides=(1,),
        padding=((SSD_CONV // 2, SSD_CONV // 2),),
        dimension_numbers=('NWC', 'WIO', 'NWC'), feature_group_count=x.shape[-1])
    return y + b.astype(x.dtype)


def rope_2d(t):
    L = t.shape[1]
    rows = L // GRID_W
    r, col = jnp.meshgrid(jnp.arange(rows), jnp.arange(GRID_W), indexing='ij')
    n_freq = RET_QK_DIM // 4
    inv = ROPE_BASE ** (-jnp.arange(n_freq, dtype=F32) / n_freq)
    ang = jnp.concatenate([r.reshape(-1, 1).astype(F32) * inv, col.reshape(-1, 1).astype(F32) * inv], axis=-1)
    cos = jnp.cos(ang)[None, :, None, :]
    sin = jnp.sin(ang)[None, :, None, :]
    tf = t.astype(F32)
    t1, t2 = tf[..., :RET_QK_DIM // 2], tf[..., RET_QK_DIM // 2:]
    return jnp.concatenate([t1 * cos - t2 * sin, t1 * sin + t2 * cos], axis=-1).astype(t.dtype)


def chunked_scan(q, k, v, log_a, s0):
    bsz, L, G, N = q.shape
    Hg, P = v.shape[3], v.shape[4]
    nc = L // CHUNK

    def to_chunks(t):
        return jnp.moveaxis(t.astype(F32).reshape((bsz, nc, CHUNK) + t.shape[2:]), 1, 0)

    tril = jnp.tril(jnp.ones((CHUNK, CHUNK), dtype=bool))[None, :, :, None, None]

    def step(h, inp):
        qc, kc, vc, ac = inp
        cs = jnp.cumsum(ac, axis=1)
        decay = jnp.exp(jnp.where(tril, cs[:, :, None] - cs[:, None, :], -jnp.inf))
        scores = jnp.einsum('bign,bjgn->bijg', qc, kc)
        y = jnp.einsum('bijgh,bjghp->bighp', scores[..., None] * decay, vc)
        y = y + jnp.exp(cs)[..., None] * jnp.einsum('bign,bghnp->bighp', qc, h)
        last = cs[:, -1]
        kv = jnp.einsum('bjgn,bjghp->bghnp', kc, vc * jnp.exp(last[:, None] - cs)[..., None])
        h = jnp.exp(last)[..., None, None] * h + kv
        return h, y

    h_last, ys = lax.scan(step, s0.astype(F32), (to_chunks(q), to_chunks(k), to_chunks(v), to_chunks(log_a)))
    y = jnp.moveaxis(ys, 0, 1).reshape((bsz, L, G, Hg, P))
    return y, h_last


def directional_scan(qc, kc, vc, ac, ql, kl, vl, al, reverse):
    if reverse:
        qc, kc, vc, ac, ql, kl, vl, al = [jnp.flip(t, axis=1) for t in (qc, kc, vc, ac, ql, kl, vl, al)]
    bsz, G, Hg, P = vc.shape[0], vc.shape[2], vc.shape[3], vc.shape[4]
    N = qc.shape[-1]
    s0 = jnp.zeros((bsz, G, Hg, N, P), F32)
    yc, s_ctx = chunked_scan(qc, kc, vc, ac, s0)
    yl, _ = chunked_scan(ql, kl, vl, al, s_ctx)
    if reverse:
        yc, yl = jnp.flip(yc, axis=1), jnp.flip(yl, axis=1)
    return yc, yl


def ssd_branch(z, xbc, dt_raw, n_ctx, conv_w, conv_b, dt_bias, a_log, d_skip, norm_w):
    xbc = jax.nn.silu(jnp.concatenate([centred_dwconv(xbc[:, :n_ctx], conv_w, conv_b),
                                       centred_dwconv(xbc[:, n_ctx:], conv_w, conv_b)], axis=1))
    bsz, T = xbc.shape[:2]
    hg = SSD_HEADS // SSD_GROUPS
    xs, bm, cm = split_cols(xbc, (SSD_INNER, SSD_GROUPS * SSD_STATE, SSD_GROUPS * SSD_STATE))
    xs = xs.reshape(bsz, T, SSD_HEADS, SSD_HEAD_DIM).astype(F32)
    bm = bm.reshape(bsz, T, SSD_GROUPS, SSD_STATE)
    cm = cm.reshape(bsz, T, SSD_GROUPS, SSD_STATE)
    y = d_skip.astype(F32)[:, None] * xs
    for d in range(2):
        dt = jax.nn.softplus(dt_raw.astype(F32) + dt_bias[d].astype(F32))
        la = (dt * -jnp.exp(a_log[d].astype(F32))).reshape(bsz, T, SSD_GROUPS, hg)
        v = (xs * dt[..., None]).reshape(bsz, T, SSD_GROUPS, hg, SSD_HEAD_DIM)
        yc, yl = directional_scan(cm[:, :n_ctx], bm[:, :n_ctx], v[:, :n_ctx], la[:, :n_ctx],
                                  cm[:, n_ctx:], bm[:, n_ctx:], v[:, n_ctx:], la[:, n_ctx:], reverse=(d == 1))
        y = y + jnp.concatenate([yc, yl], axis=1).reshape(bsz, T, SSD_HEADS, SSD_HEAD_DIM)
    yg = (y.reshape(bsz, T, SSD_INNER) * jax.nn.silu(z.astype(F32))).reshape(bsz, T, SSD_GROUPS, SSD_INNER // SSD_GROUPS)
    yg = yg * lax.rsqrt(jnp.mean(yg * yg, axis=-1, keepdims=True) + EPS)
    return (yg.reshape(bsz, T, SSD_INNER) * norm_w.astype(F32)).astype(z.dtype)


def ret_branch(q, k, v, g, n_ctx, decay_logit, gn_w):
    bsz, T = q.shape[:2]
    q = q.reshape(bsz, T, RET_HEADS, RET_QK_DIM)
    k = k.reshape(bsz, T, RET_HEADS, RET_QK_DIM) * (RET_QK_DIM ** -0.5)
    v = v.reshape(bsz, T, RET_HEADS, 1, RET_V_DIM)
    qc, kc = q[:, :n_ctx], k[:, :n_ctx]
    ql, kl = rope_2d(q[:, n_ctx:]), rope_2d(k[:, n_ctx:])
    ys = []
    for d in range(2):
        la = jax.nn.log_sigmoid(decay_logit[d].astype(F32))[:, None]
        lac = jnp.broadcast_to(la, (bsz, n_ctx, RET_HEADS, 1))
        lal = jnp.broadcast_to(la, (bsz, T - n_ctx, RET_HEADS, 1))
        yc, yl = directional_scan(qc, kc, v[:, :n_ctx], lac, ql, kl, v[:, n_ctx:], lal, reverse=(d == 1))
        ys.append(jnp.concatenate([yc, yl], axis=1))
    y = (ys[0] + ys[1]).reshape(bsz, T, RET_HEADS, RET_V_DIM)
    mu = jnp.mean(y, axis=-1, keepdims=True)
    yc0 = y - mu
    y = yc0 * lax.rsqrt(jnp.mean(yc0 * yc0, axis=-1, keepdims=True) + EPS)
    y = y.reshape(bsz, T, RET_V_WIDTH) * gn_w.astype(F32)
    return (jax.nn.silu(g.astype(F32)) * y).astype(g.dtype)


def hybrid_mixer(h_c, h_l, w_in, conv_w, conv_b, dt_bias, a_log, d_skip, ssd_norm_w, ret_decay, ret_gn_w,
                 w_ssd_o, w_ret_o, w_o):
    n_ctx = h_c.shape[1]
    p = jnp.concatenate([h_c, h_l], axis=1) @ w_in
    z, xbc, dt_raw, q, k, v, g, gate_s, gate_r = split_cols(p, PROJ_SIZES)
    y_s = ssd_branch(z, xbc, dt_raw, n_ctx, conv_w, conv_b, dt_bias, a_log, d_skip, ssd_norm_w) @ w_ssd_o
    y_r = ret_branch(q, k, v, g, n_ctx, ret_decay, ret_gn_w) @ w_ret_o
    m = jax.nn.sigmoid(gate_s) * y_s + jax.nn.sigmoid(gate_r) * y_r
    out = m @ w_o
    return out[:, :n_ctx], out[:, n_ctx:]


def expert_choice_ffn(h, w_router, w_gate, w_up, w_down):
    bsz, L, _ = h.shape
    cap = CAPACITY_FACTOR * L // N_EXPERTS
    aff = jax.nn.softmax((h @ w_router).astype(F32), axis=-1)
    gsel, idx = lax.top_k(jnp.swapaxes(aff, 1, 2), cap)
    bidx = jnp.arange(bsz)[:, None, None]
    xin = h[bidx, idx]
    hid = jax.nn.silu(jnp.einsum('becd,edf->becf', xin, w_gate)) * jnp.einsum('becd,edf->becf', xin, w_up)
    out = jnp.einsum('becf,efd->becd', hid, w_down) * gsel[..., None].astype(h.dtype)
    return jnp.zeros_like(h).at[bidx, idx].add(out)


def setup_inputs(seed: int = 0) -> dict:
    key = jax.random.key(seed)
    ks = jax.random.split(key, 26)
    D = D_MODEL

    def nrm(k, shape, scale):
        return jax.random.normal(k, shape, F32) * scale

    x = nrm(ks[0], (BATCH, SEQ, D), 1.0)
    c = nrm(ks[1], (BATCH, D), 1.0)
    ctx = nrm(ks[2], (BATCH, CTX_LEN, D), 1.0)
    c_ctx = nrm(ks[3], (D,), 1.0)
    w_mod = nrm(ks[4], (DEPTH, D, 6 * D), D ** -0.5)
    b_mod = nrm(ks[5], (DEPTH, 6 * D), 0.01)
    norm_mix_w = 1.0 + nrm(ks[6], (DEPTH, D), 0.01)
    w_in = nrm(ks[7], (DEPTH, D, PROJ_DIM), D ** -0.5)
    conv_w = nrm(ks[8], (DEPTH, SSD_CONV, XBC_DIM), SSD_CONV ** -0.5)
    conv_b = nrm(ks[9], (DEPTH, XBC_DIM), 0.01)
    dt0 = jnp.exp(jax.random.uniform(ks[10], (DEPTH, 2, SSD_HEADS), F32, math.log(1e-3), math.log(1e-1)))
    ssd_dt_bias = dt0 + jnp.log(-jnp.expm1(-dt0))
    ssd_a_log = jnp.log(jax.random.uniform(ks[11], (DEPTH, 2, SSD_HEADS), F32, 1.0, 16.0))
    ssd_d = 1.0 + nrm(ks[12], (DEPTH, SSD_HEADS), 0.1)
    ssd_norm_w = 1.0 + nrm(ks[13], (DEPTH, SSD_INNER), 0.01)
    gamma0 = 1.0 - 2.0 ** (-5.0 - jnp.arange(RET_HEADS, dtype=F32))
    ret_decay = jnp.log(gamma0 / (1.0 - gamma0)) + nrm(ks[14], (DEPTH, 2, RET_HEADS), 0.01)
    ret_gn_w = 1.0 + nrm(ks[15], (DEPTH, RET_V_WIDTH), 0.01)
    w_ssd_o = nrm(ks[16], (DEPTH, SSD_INNER, D), SSD_INNER ** -0.5)
    w_ret_o = nrm(ks[17], (DEPTH, RET_V_WIDTH, D), RET_V_WIDTH ** -0.5)
    w_o = nrm(ks[18], (DEPTH, D, D), D ** -0.5)
    norm_ffn_w = 1.0 + nrm(ks[19], (DEPTH, D), 0.01)
    w_router = nrm(ks[20], (DEPTH, D, N_EXPERTS), D ** -0.5)
    w_gate = nrm(ks[21], (DEPTH, N_EXPERTS, D, EXPERT_FF), D ** -0.5)
    w_up = nrm(ks[22], (DEPTH, N_EXPERTS, D, EXPERT_FF), D ** -0.5)
    w_down = nrm(ks[23], (DEPTH, N_EXPERTS, EXPERT_FF, D), EXPERT_FF ** -0.5)
    final_norm_w = 1.0 + nrm(ks[24], (D,), 0.01)
    return {'x': x, 'c': c, 'ctx': ctx, 'c_ctx': c_ctx, 'w_mod': w_mod, 'b_mod': b_mod,
            'norm_mix_w': norm_mix_w, 'w_in': w_in, 'conv_w': conv_w, 'conv_b': conv_b,
            'ssd_dt_bias': ssd_dt_bias, 'ssd_a_log': ssd_a_log, 'ssd_d': ssd_d, 'ssd_norm_w': ssd_norm_w,
            'ret_decay': ret_decay, 'ret_gn_w': ret_gn_w, 'w_ssd_o': w_ssd_o, 'w_ret_o': w_ret_o, 'w_o': w_o,
            'norm_ffn_w': norm_ffn_w, 'w_router': w_router, 'w_gate': w_gate, 'w_up': w_up, 'w_down': w_down,
            'final_norm_w': final_norm_w}


def reference(x, c, ctx, c_ctx, w_mod, b_mod, norm_mix_w, w_in, conv_w, conv_b, ssd_dt_bias, ssd_a_log, ssd_d,
              ssd_norm_w, ret_decay, ret_gn_w, w_ssd_o, w_ret_o, w_o, norm_ffn_w, w_router, w_gate, w_up, w_down,
              final_norm_w):
    bsz = x.shape[0]
    h_ctx = ctx
    for l in range(DEPTH):
        last = l == DEPTH - 1
        mod = (jax.nn.silu(c) @ w_mod[l] + b_mod[l]).reshape(bsz, 6, 1, D_MODEL)
        mod_c = (jax.nn.silu(c_ctx) @ w_mod[l] + b_mod[l]).reshape(6, 1, 1, D_MODEL)
        hl = modulate(rmsnorm(x, norm_mix_w[l]), mod[:, 0], mod[:, 1])
        hc = modulate(rmsnorm(h_ctx, norm_mix_w[l]), mod_c[0], mod_c[1])
        mix_c, mix_l = hybrid_mixer(hc, hl, w_in[l], conv_w[l], conv_b[l], ssd_dt_bias[l], ssd_a_log[l], ssd_d[l],
                                    ssd_norm_w[l], ret_decay[l], ret_gn_w[l], w_ssd_o[l], w_ret_o[l], w_o[l])
        x = x + mod[:, 2] * mix_l
        hl2 = modulate(rmsnorm(x, norm_ffn_w[l]), mod[:, 3], mod[:, 4])
        x = x + mod[:, 5] * expert_choice_ffn(hl2, w_router[l], w_gate[l], w_up[l], w_down[l])
        if not last:
            h_ctx = h_ctx + mod_c[2] * mix_c
            hc2 = modulate(rmsnorm(h_ctx, norm_ffn_w[l]), mod_c[3], mod_c[4])
            h_ctx = h_ctx + mod_c[5] * expert_choice_ffn(hc2, w_router[l], w_gate[l], w_up[l], w_down[l])
    return rmsnorm(x, final_norm_w)
```

```python
import functools
import math

import jax
import jax.numpy as jnp
from jax import lax
from jax.experimental import pallas as pl
from jax.experimental.pallas import tpu as pltpu

F32 = jnp.float32
BF16 = jnp.bfloat16
I32 = jnp.int32
HIGHEST = lax.Precision.HIGHEST

EPS = 1e-6
CHUNK = 128
GRID_W = 64
ROPE_BASE = 10000.0

SSD_HEAD_DIM = 64
SSD_GROUPS = 4
SSD_STATE = 128
SSD_CONV = 3
RET_HEADS = 8
N_EXPERTS = 16
CAPACITY_FACTOR = 2

LANES = 128
BF16_ROWS = 16
VMEM_LIMIT = 56 * 1024 * 1024
NEG_BIG = -1e30
I32_MAX = 2**31 - 1


def _cparams(*sem):
    return pltpu.CompilerParams(dimension_semantics=sem, vmem_limit_bytes=VMEM_LIMIT)


def _sigmoid(x):
    return jax.nn.sigmoid(x)


def _silu(x):
    return x * _sigmoid(x)


def _softplus(x):
    return jnp.maximum(x, 0.0) + jnp.log1p(jnp.exp(-jnp.abs(x)))


def _dot(a, b):
    return jnp.dot(a, b, preferred_element_type=F32)


def _dot_nt(a, b):
    return lax.dot_general(a, b, (((1,), (1,)), ((), ())), preferred_element_type=F32)


def _mod_kernel(c_ref, w_ref, b_ref, o_ref):
    s = _silu(c_ref[...])
    o_ref[0] = jnp.dot(s, w_ref[0], precision=HIGHEST, preferred_element_type=F32) + b_ref[0]


def _mod_call(cc, w_mod, b_mod):
    depth, d, n = w_mod.shape
    tn = 1024
    rows = cc.shape[0]
    return pl.pallas_call(
        _mod_kernel,
        grid=(depth, n // tn),
        in_specs=[pl.BlockSpec((rows, d), lambda l, j: (0, 0)),
                  pl.BlockSpec((1, d, tn), lambda l, j: (l, 0, j)),
                  pl.BlockSpec((1, 1, tn), lambda l, j: (l, 0, j))],
        out_specs=pl.BlockSpec((1, rows, tn), lambda l, j: (l, 0, j)),
        out_shape=jax.ShapeDtypeStruct((depth, rows, n), F32),
        compiler_params=_cparams("parallel", "parallel"),
        name="mod",
    )(cc, w_mod, b_mod.reshape(depth, 1, n))


def _inproj_kernel(x_ref, nw_ref, sh_ref, sc_ref, w_ref, wdt_ref, p_ref, dt_ref, hn_ref):
    @pl.when(pl.program_id(2) == 0)
    def _():
        x = x_ref[0]
        y = x * lax.rsqrt(jnp.mean(x * x, axis=-1, keepdims=True) + EPS) * nw_ref[...]
        hb = (y * (1.0 + sc_ref[0]) + sh_ref[0]).astype(BF16)
        hn_ref[...] = hb
        dt_ref[0] = _dot(hb, wdt_ref[...])

    p_ref[0] = _dot(hn_ref[...], w_ref[...]).astype(BF16)


def _inproj_call(x, nw, shift, scale, w_main, w_dt, tm):
    bsz, L, d = x.shape
    n = w_main.shape[1]
    tn = 1024
    return pl.pallas_call(
        _inproj_kernel,
        grid=(bsz, L // tm, n // tn),
        in_specs=[pl.BlockSpec((1, tm, d), lambda b, i, j: (b, i, 0)),
                  pl.BlockSpec((1, d), lambda b, i, j: (0, 0)),
                  pl.BlockSpec((1, 1, d), lambda b, i, j: (b, 0, 0)),
                  pl.BlockSpec((1, 1, d), lambda b, i, j: (b, 0, 0)),
                  pl.BlockSpec((d, tn), lambda b, i, j: (0, j)),
                  pl.BlockSpec((d, LANES), lambda b, i, j: (0, 0))],
        out_specs=[pl.BlockSpec((1, tm, tn), lambda b, i, j: (b, i, j)),
                   pl.BlockSpec((1, tm, LANES), lambda b, i, j: (b, i, 0))],
        out_shape=[jax.ShapeDtypeStruct((bsz, L, n), BF16),
                   jax.ShapeDtypeStruct((bsz, L, LANES), F32)],
        scratch_shapes=[pltpu.VMEM((tm, d), BF16)],
        compiler_params=_cparams("parallel", "parallel", "arbitrary"),
        name="inproj",
    )(x, nw, shift, scale, w_main, w_dt)


P_Z, P_X, P_V, P_G, P_B, P_C, P_Q, P_K, P_GS, P_GR, P_END = (
    0, 2048, 4096, 6144, 8192, 8704, 9216, 10240, 11264, 12288, 13312)


def _conv_kernel(cur_ref, prev_ref, next_ref, w_ref, b_ref, o_ref, *, n_tiles):
    i = pl.program_id(1)
    x = cur_ref[0].astype(F32)
    tl = x.shape[0]
    prow = jnp.where(i > 0, prev_ref[0][BF16_ROWS - 1:BF16_ROWS, :].astype(F32), 0.0)
    nrow = jnp.where(i < n_tiles - 1, next_ref[0][0:1, :].astype(F32), 0.0)
    rid = lax.broadcasted_iota(I32, x.shape, 0)
    xp = jnp.where(rid == 0, prow, pltpu.roll(x, 1, 0))
    xn = jnp.where(rid == tl - 1, nrow, pltpu.roll(x, tl - 1, 0))
    w = w_ref[...]
    y = xp * w[0:1] + x * w[1:2] + xn * w[2:3] + b_ref[...]
    o_ref[0] = _silu(y).astype(BF16)


def _conv_call(p, conv_w, conv_b, tl):
    bsz, L, _ = p.shape
    tc = 1024
    nc = conv_w.shape[1] // tc
    n_tiles = L // tl
    rpb = tl // BF16_ROWS
    last = L // BF16_ROWS - 1

    def col(j):
        return jnp.where(j < 2, P_X // tc + j, P_B // tc)

    return pl.pallas_call(
        functools.partial(_conv_kernel, n_tiles=n_tiles),
        grid=(bsz, n_tiles, nc),
        in_specs=[pl.BlockSpec((1, tl, tc), lambda b, i, j: (b, i, col(j))),
                  pl.BlockSpec((1, BF16_ROWS, tc), lambda b, i, j: (b, jnp.maximum(i * rpb - 1, 0), col(j))),
                  pl.BlockSpec((1, BF16_ROWS, tc), lambda b, i, j: (b, jnp.minimum((i + 1) * rpb, last), col(j))),
                  pl.BlockSpec((SSD_CONV, tc), lambda b, i, j: (0, j)),
                  pl.BlockSpec((1, tc), lambda b, i, j: (0, j))],
        out_specs=pl.BlockSpec((1, tl, tc), lambda b, i, j: (b, i, j)),
        out_shape=jax.ShapeDtypeStruct((bsz, L, conv_w.shape[1]), BF16),
        compiler_params=_cparams("parallel", "parallel", "parallel"),
        name="conv",
    )(p, p, p, conv_w, conv_b)


def _ssd_kernel(*refs, rev, final):
    if final:
        (x_ref, b_ref, c_ref, dt_ref, dtb_ref, alog_ref, hexp_ref, s0_ref,
         yf_ref, z_ref, dskip_ref, nw_ref, y_ref, sT_ref, st_ref) = refs
    else:
        (x_ref, b_ref, c_ref, dt_ref, dtb_ref, alog_ref, hexp_ref, s0_ref,
         y_ref, sT_ref, st_ref) = refs
    c = pl.program_id(1)
    T = CHUNK
    N = SSD_STATE
    P = SSD_HEAD_DIM
    hg = st_ref.shape[1] // P

    @pl.when(c == 0)
    def _():
        st_ref[...] = s0_ref[0]

    dtv = _softplus(dt_ref[0] + dtb_ref[...])
    la = dtv * (-jnp.exp(alog_ref[...]))
    ii = lax.broadcasted_iota(I32, (T, T), 0)
    jj = lax.broadcasted_iota(I32, (T, T), 1)
    tri = (jj >= ii) if rev else (jj <= ii)
    cs = jnp.dot(tri.astype(F32), la, precision=HIGHEST, preferred_element_type=F32)
    csT = cs.T
    dtT = dtv.T
    last = 0 if rev else T - 1
    ecs = jnp.exp(cs)
    wT = dtT * jnp.exp(csT[:, last:last + 1] - csT)
    etot = jnp.exp(cs[last:last + 1, :])
    dexp = jnp.dot(jnp.broadcast_to(etot, (8, LANES)), hexp_ref[...], precision=HIGHEST,
                   preferred_element_type=F32)[0:1]

    xall = x_ref[0]
    ys = []
    for g in range(SSD_GROUPS):
        cg = c_ref[0][:, g * N:(g + 1) * N]
        bg = b_ref[0][:, g * N:(g + 1) * N]
        scores = _dot_nt(cg, bg)
        cgf = cg.astype(F32)
        bgT = bg.astype(F32).T
        yh, kvh = [], []
        for hh in range(hg):
            h = g * hg + hh
            col = cs[:, h:h + 1]
            row = csT[h:h + 1, :]
            decay = jnp.exp(jnp.where(tri, col - row, NEG_BIG))
            m = (scores * decay * dtT[h:h + 1, :]).astype(BF16)
            ce = (cgf * ecs[:, h:h + 1]).astype(BF16)
            xh = xall[:, h * P:(h + 1) * P]
            sh = st_ref[g * N:(g + 1) * N, hh * P:(hh + 1) * P].astype(BF16)
            yh.append(_dot(jnp.concatenate([m, ce], axis=1), jnp.concatenate([xh, sh], axis=0)))
            kvh.append(_dot((bgT * wT[h:h + 1, :]).astype(BF16), xh))
        ys.append(jnp.concatenate(yh, axis=1))
        kv = jnp.concatenate(kvh, axis=1)
        w = hg * P
        st_ref[g * N:(g + 1) * N, :] = st_ref[g * N:(g + 1) * N, :] * dexp[:, g * w:(g + 1) * w] + kv

    y = jnp.concatenate(ys, axis=1)
    if final:
        y = dskip_ref[...] * xall.astype(F32) + yf_ref[0] + y
        yg = y * _silu(z_ref[0].astype(F32))
        w = hg * P
        outs = []
        for g in range(SSD_GROUPS):
            t = yg[:, g * w:(g + 1) * w]
            outs.append(t * lax.rsqrt(jnp.mean(t * t, axis=-1, keepdims=True) + EPS))
        y_ref[0] = (jnp.concatenate(outs, axis=1) * nw_ref[...]).astype(y_ref.dtype)
    else:
        y_ref[0] = y

    @pl.when(c == pl.num_programs(1) - 1)
    def _():
        sT_ref[0] = st_ref[...]


def _ssd_call(xbc, dt, p, dtb, alog, hexp, s0, rev, final_args=None):
    bsz, L, _ = xbc.shape
    nch = L // CHUNK
    inner = hexp.shape[1]
    srows = SSD_GROUPS * SSD_STATE
    scols = inner // SSD_GROUPS
    final = final_args is not None

    def tok(b, c):
        return (b, nch - 1 - c, 0) if rev else (b, c, 0)

    def tokcol(blk):
        return lambda b, c: tok(b, c)[:2] + (blk,)

    gn = SSD_GROUPS * SSD_STATE
    in_specs = [pl.BlockSpec((1, CHUNK, inner), tokcol(0)),
                pl.BlockSpec((1, CHUNK, gn), tokcol(inner // gn)),
                pl.BlockSpec((1, CHUNK, gn), tokcol(inner // gn + 1)),
                pl.BlockSpec((1, CHUNK, LANES), tok),
                pl.BlockSpec((1, LANES), lambda b, c: (0, 0)),
                pl.BlockSpec((1, LANES), lambda b, c: (0, 0)),
                pl.BlockSpec((LANES, inner), lambda b, c: (0, 0)),
                pl.BlockSpec((1, srows, scols), lambda b, c: (b, 0, 0))]
    args = [xbc, xbc, xbc, dt, dtb, alog, hexp, s0]
    if final:
        yf, dskip, nw = final_args
        in_specs += [pl.BlockSpec((1, CHUNK, inner), tok),
                     pl.BlockSpec((1, CHUNK, inner), tokcol(P_Z // inner)),
                     pl.BlockSpec((1, inner), lambda b, c: (0, 0)),
                     pl.BlockSpec((1, inner), lambda b, c: (0, 0))]
        args += [yf, p, dskip, nw]
    return pl.pallas_call(
        functools.partial(_ssd_kernel, rev=rev, final=final),
        grid=(bsz, nch),
        in_specs=in_specs,
        out_specs=[pl.BlockSpec((1, CHUNK, inner), tok),
                   pl.BlockSpec((1, srows, scols), lambda b, c: (b, 0, 0))],
        out_shape=[jax.ShapeDtypeStruct((bsz, L, inner), BF16 if final else F32),
                   jax.ShapeDtypeStruct((bsz, srows, scols), F32)],
        scratch_shapes=[pltpu.VMEM((srows, scols), F32)],
        compiler_params=_cparams("parallel", "arbitrary"),
        name="ssd_rev" if rev else "ssd_fwd",
    )(*args)


def _ret_kernel(*refs, rev, final, rope):
    refs = list(refs)
    q_ref, k_ref, v_ref, lg_ref = refs[:4]
    refs = refs[4:]
    if rope:
        cos_ref, sin_ref = refs[:2]
        refs = refs[2:]
    s0_ref = refs[0]
    refs = refs[1:]
    if final:
        yf_ref, g_ref, gnw_ref = refs[:3]
        refs = refs[3:]
    y_ref, sT_ref, st_ref = refs
    c = pl.program_id(1)
    T = CHUNK
    dk = q_ref.shape[2] // RET_HEADS
    dv = v_ref.shape[2] // RET_HEADS

    @pl.when(c == 0)
    def _():
        st_ref[...] = s0_ref[0]

    ii = lax.broadcasted_iota(I32, (T, T), 0)
    jj = lax.broadcasted_iota(I32, (T, T), 1)
    tri = (jj >= ii) if rev else (jj <= ii)
    dist = ((jj - ii) if rev else (ii - jj)).astype(F32)
    ri = lax.broadcasted_iota(I32, (T, dk), 0).astype(F32)
    n_in = (T - ri) if rev else (ri + 1.0)
    n_out = ri if rev else (T - 1.0 - ri)
    kscale = dk ** -0.5
    qa = q_ref[0]
    ka = k_ref[0]
    va = v_ref[0]
    ys = []
    for h in range(RET_HEADS):
        la2 = -_softplus(-lg_ref[h:h + 1, :])
        la = la2[:, :dk]
        qf = qa[:, h * dk:(h + 1) * dk].astype(F32)
        kf = ka[:, h * dk:(h + 1) * dk].astype(F32)
        if rope:
            qf = qf * cos_ref[...] + pltpu.roll(qf, dk // 2, 1) * sin_ref[...]
            kf = kf * cos_ref[...] + pltpu.roll(kf, dk // 2, 1) * sin_ref[...]
        kf = kf * kscale
        scores = _dot_nt(qf.astype(BF16), kf.astype(BF16))
        decay = jnp.exp(jnp.where(tri, dist * la, NEG_BIG))
        m = (scores * decay).astype(BF16)
        qe = (qf * jnp.exp(n_in * la)).astype(BF16)
        vh = va[:, h * dv:(h + 1) * dv]
        sh = st_ref[h * dk:(h + 1) * dk, :]
        ys.append(_dot(jnp.concatenate([m, qe], axis=1), jnp.concatenate([vh, sh.astype(BF16)], axis=0)))
        kw = (kf * jnp.exp(n_out * la)).T.astype(BF16)
        st_ref[h * dk:(h + 1) * dk, :] = sh * jnp.exp(float(T) * la2) + _dot(kw, vh)

    if final:
        outs = []
        for h in range(RET_HEADS):
            t = ys[h] + yf_ref[0][:, h * dv:(h + 1) * dv]
            t = t - jnp.mean(t, axis=-1, keepdims=True)
            outs.append(t * lax.rsqrt(jnp.mean(t * t, axis=-1, keepdims=True) + EPS))
        y = jnp.concatenate(outs, axis=1) * gnw_ref[...]
        y_ref[0] = (_silu(g_ref[0].astype(F32)) * y).astype(y_ref.dtype)
    else:
        y_ref[0] = jnp.concatenate(ys, axis=1)

    @pl.when(c == pl.num_programs(1) - 1)
    def _():
        sT_ref[0] = st_ref[...]


def _ret_call(p, lg, s0, rev, rope_tabs=None, final_args=None):
    bsz, L, _ = p.shape
    nch = L // CHUNK
    qk = P_K - P_Q
    vw = P_G - P_V
    dk = qk // RET_HEADS
    dv = vw // RET_HEADS
    final = final_args is not None
    rope = rope_tabs is not None

    def tok(b, c):
        return (b, nch - 1 - c, 0) if rev else (b, c, 0)

    def tokcol(blk):
        return lambda b, c: tok(b, c)[:2] + (blk,)

    def chunk_only(b, c):
        return tok(b, c)[1:]

    in_specs = [pl.BlockSpec((1, CHUNK, qk), tokcol(P_Q // qk)),
                pl.BlockSpec((1, CHUNK, qk), tokcol(P_K // qk)),
                pl.BlockSpec((1, CHUNK, vw), tokcol(P_V // vw)),
                pl.BlockSpec((RET_HEADS, dv), lambda b, c: (0, 0))]
    args = [p, p, p, lg]
    if rope:
        in_specs += [pl.BlockSpec((CHUNK, dk), chunk_only), pl.BlockSpec((CHUNK, dk), chunk_only)]
        args += list(rope_tabs)
    in_specs += [pl.BlockSpec((1, RET_HEADS * dk, dv), lambda b, c: (b, 0, 0))]
    args += [s0]
    if final:
        yf, gnw = final_args
        in_specs += [pl.BlockSpec((1, CHUNK, vw), tok),
                     pl.BlockSpec((1, CHUNK, vw), tokcol(P_G // vw)),
                     pl.BlockSpec((1, vw), lambda b, c: (0, 0))]
        args += [yf, p, gnw]
    return pl.pallas_call(
        functools.partial(_ret_kernel, rev=rev, final=final, rope=rope),
        grid=(bsz, nch),
        in_specs=in_specs,
        out_specs=[pl.BlockSpec((1, CHUNK, vw), tok),
                   pl.BlockSpec((1, RET_HEADS * dk, dv), lambda b, c: (b, 0, 0))],
        out_shape=[jax.ShapeDtypeStruct((bsz, L, vw), BF16 if final else F32),
                   jax.ShapeDtypeStruct((bsz, RET_HEADS * dk, dv), F32)],
        scratch_shapes=[pltpu.VMEM((RET_HEADS * dk, dv), F32)],
        compiler_params=_cparams("parallel", "arbitrary"),
        name="ret_rev" if rev else "ret_fwd",
    )(*args)


def _mixout_kernel(ys_ref, yr_ref, gs_ref, gr_ref, x_ref, gate_ref, nw_ref, sh_ref, sc_ref,
                   wso_ref, wro_ref, wo_ref, wr_ref, xn_ref, h2_ref, aff_ref, affT_ref):
    a = _dot(ys_ref[0], wso_ref[...])
    b = _dot(yr_ref[0], wro_ref[...])
    m = _sigmoid(gs_ref[0].astype(F32)) * a + _sigmoid(gr_ref[0].astype(F32)) * b
    xn = x_ref[0] + gate_ref[0] * _dot(m.astype(BF16), wo_ref[...])
    xn_ref[0] = xn
    y = xn * lax.rsqrt(jnp.mean(xn * xn, axis=-1, keepdims=True) + EPS) * nw_ref[...]
    h2 = y * (1.0 + sc_ref[0]) + sh_ref[0]
    h2_ref[0] = h2.astype(BF16)
    logits = jnp.dot(h2, wr_ref[...], precision=HIGHEST, preferred_element_type=F32)
    lane = lax.broadcasted_iota(I32, logits.shape, 1)
    logits = jnp.where(lane < N_EXPERTS, logits, NEG_BIG)
    e = jnp.exp(logits - jnp.max(logits, axis=-1, keepdims=True))
    aff = e / jnp.sum(e, axis=-1, keepdims=True)
    aff_ref[0] = aff
    affT_ref[0] = aff.T[:N_EXPERTS, :]


def _mixout_call(ys, yr, p, x, gate, nw, shift, scale, wso, wro, wo, wr, tm):
    bsz, L, d = x.shape
    inner = ys.shape[2]
    vw = yr.shape[2]
    const = lambda b, i: (0, 0)
    vec = lambda b, i: (b, 0, 0)
    return pl.pallas_call(
        _mixout_kernel,
        grid=(bsz, L // tm),
        in_specs=[pl.BlockSpec((1, tm, inner), lambda b, i: (b, i, 0)),
                  pl.BlockSpec((1, tm, vw), lambda b, i: (b, i, 0)),
                  pl.BlockSpec((1, tm, d), lambda b, i: (b, i, P_GS // d)),
                  pl.BlockSpec((1, tm, d), lambda b, i: (b, i, P_GR // d)),
                  pl.BlockSpec((1, tm, d), lambda b, i: (b, i, 0)),
                  pl.BlockSpec((1, 1, d), vec),
                  pl.BlockSpec((1, d), const),
                  pl.BlockSpec((1, 1, d), vec),
                  pl.BlockSpec((1, 1, d), vec),
                  pl.BlockSpec((inner, d), const, pipeline_mode=pl.Buffered(1)),
                  pl.BlockSpec((vw, d), const, pipeline_mode=pl.Buffered(1)),
                  pl.BlockSpec((d, d), const, pipeline_mode=pl.Buffered(1)),
                  pl.BlockSpec((d, LANES), const, pipeline_mode=pl.Buffered(1))],
        out_specs=[pl.BlockSpec((1, tm, d), lambda b, i: (b, i, 0)),
                   pl.BlockSpec((1, tm, d), lambda b, i: (b, i, 0)),
                   pl.BlockSpec((1, tm, LANES), lambda b, i: (b, i, 0)),
                   pl.BlockSpec((1, N_EXPERTS, tm), lambda b, i: (b, 0, i))],
        out_shape=[jax.ShapeDtypeStruct((bsz, L, d), F32),
                   jax.ShapeDtypeStruct((bsz, L, d), BF16),
                   jax.ShapeDtypeStruct((bsz, L, LANES), F32),
                   jax.ShapeDtypeStruct((bsz, N_EXPERTS, L), F32)],
        compiler_params=_cparams("parallel", "parallel"),
        name="mixout",
    )(ys, yr, p, p, x, gate, nw, shift, scale, wso, wro, wo, wr)


def _route_kernel(affT_ref, ind_ref, thr_ref, need_ref, gtc_ref, eqc_ref, *, cap):
    bits = pltpu.bitcast(affT_ref[0], I32)

    def body(i, t):
        cand = t | lax.shift_left(jnp.int32(1), 30 - i)
        cnt = jnp.sum(jnp.where(bits >= cand, 1.0, 0.0), axis=1, keepdims=True)
        return jnp.where(cnt >= cap, cand, t)

    thr = lax.fori_loop(0, 31, body, jnp.zeros((N_EXPERTS, 1), I32))
    gt = jnp.where(bits > thr, 1.0, 0.0)
    eq = jnp.where(bits == thr, 1.0, 0.0)
    need = cap - jnp.sum(gt, axis=1, keepdims=True)
    thr_ref[0] = jnp.broadcast_to(thr, (N_EXPERTS, LANES))
    need_ref[0] = jnp.broadcast_to(need, (N_EXPERTS, LANES)).astype(I32)
    gtc_ref[0] = _dot(gt.astype(BF16), ind_ref[...]).astype(I32)
    eqc_ref[0] = _dot(eq.astype(BF16), ind_ref[...]).astype(I32)


def _route_call(affT, tt, cap):
    bsz, ne, L = affT.shape
    ind = (jnp.arange(L, dtype=I32)[:, None] // tt == jnp.arange(LANES, dtype=I32)[None, :]).astype(BF16)
    o = jax.ShapeDtypeStruct((bsz, ne, LANES), I32)
    ospec = pl.BlockSpec((1, ne, LANES), lambda b: (b, 0, 0))
    return pl.pallas_call(
        functools.partial(_route_kernel, cap=cap),
        grid=(bsz,),
        in_specs=[pl.BlockSpec((1, ne, L), lambda b: (b, 0, 0)),
                  pl.BlockSpec((L, LANES), lambda b: (0, 0))],
        out_specs=[ospec] * 4,
        out_shape=[o] * 4,
        compiler_params=_cparams("parallel"),
        name="route",
    )(affT, ind)


def _excl_prefix_lanes(flags8, tt):
    a = lax.broadcasted_iota(I32, (tt, tt), 0)
    b = lax.broadcasted_iota(I32, (tt, tt), 1)
    return _dot(flags8, jnp.where(a < b, 1.0, 0.0).astype(BF16))


def _ffn_kernel(thr_s, need_s, eqb_s, off_s, n_s, aff_ref, h_ref, wg_ref, wu_ref, wd_ref, o_ref, xin_ref,
                *, nk, tt, rb):
    e = pl.program_id(0)
    b = pl.program_id(1)
    k = pl.program_id(2)
    be = b * N_EXPERTS + e
    rows = xin_ref.shape[0]

    @pl.when(k == 0)
    def _():
        xin_ref[...] = jnp.zeros_like(xin_ref)

    bits = pltpu.bitcast(aff_ref[0, 0], I32)
    thr = thr_s[be]
    gt = bits > thr
    eq = bits == thr
    eq8 = jnp.broadcast_to(jnp.where(eq, 1.0, 0.0), (8, tt)).astype(BF16)
    eqrank = _excl_prefix_lanes(eq8, tt)[0:1] + eqb_s[be * nk + k].astype(F32)
    sel = jnp.where(gt, 1.0, jnp.where(eq, jnp.where(eqrank < need_s[be].astype(F32), 1.0, 0.0), 0.0))
    sel8 = jnp.broadcast_to(sel, (8, tt)).astype(BF16)
    rank = _excl_prefix_lanes(sel8, tt)[0:1]
    n = n_s[be * nk + k]
    off = off_s[be * (nk + 1) + k]
    hb = h_ref[0]
    slot = lax.broadcasted_iota(I32, (LANES, tt), 0).astype(F32)
    for sb in range(tt // LANES):
        @pl.when(sb * LANES < n)
        def _():
            onehot = jnp.where(rank == slot + float(sb * LANES), sel, 0.0).astype(BF16)
            dst = pl.multiple_of(off + sb * LANES, BF16_ROWS)
            xin_ref[pl.ds(dst, LANES), :] = _dot(onehot, hb).astype(BF16)

    @pl.when(k == nk - 1)
    def _():
        used = off_s[be * (nk + 1) + nk] + LANES
        for r in range(rows // rb):
            @pl.when(r * rb < used)
            def _():
                xb = xin_ref[r * rb:(r + 1) * rb, :]
                hid = (_silu(_dot(xb, wg_ref[0])) * _dot(xb, wu_ref[0])).astype(BF16)
                o_ref[0, 0, r * rb:(r + 1) * rb, :] = _dot(hid, wd_ref[0]).astype(BF16)

            @pl.when(r * rb >= used)
            def _():
                o_ref[0, 0, r * rb:(r + 1) * rb, :] = jnp.zeros((rb, o_ref.shape[3]), BF16)


def _ffn_call(scalars, affT4, h2, wg, wu, wd, tt, rows, rb):
    bsz, L, d = h2.shape
    ne, _, ff = wg.shape
    nk = L // tt
    grid_spec = pltpu.PrefetchScalarGridSpec(
        num_scalar_prefetch=5,
        grid=(ne, bsz, nk),
        in_specs=[pl.BlockSpec((1, 1, 1, tt), lambda e, b, k, *_: (b, e, 0, k)),
                  pl.BlockSpec((1, tt, d), lambda e, b, k, *_: (b, k, 0)),
                  pl.BlockSpec((1, d, ff), lambda e, b, k, *_: (e, 0, 0)),
                  pl.BlockSpec((1, d, ff), lambda e, b, k, *_: (e, 0, 0)),
                  pl.BlockSpec((1, ff, d), lambda e, b, k, *_: (e, 0, 0))],
        out_specs=pl.BlockSpec((1, 1, rows, d), lambda e, b, k, *_: (b, e, 0, 0)),
        scratch_shapes=[pltpu.VMEM((rows, d), BF16)])
    return pl.pallas_call(
        functools.partial(_ffn_kernel, nk=nk, tt=tt, rb=rb),
        grid_spec=grid_spec,
        out_shape=jax.ShapeDtypeStruct((bsz, ne, rows, d), BF16),
        compiler_params=_cparams("parallel", "arbitrary", "arbitrary"),
        name="ffn",
    )(*scalars, affT4, h2, wg, wu, wd)


def _combine_kernel(*refs, nk, tt, final):
    off_s, n_s, aff_ref, thr_ref, need_ref, eqb_ref, x_ref, gate_ref = refs[:8]
    refs = refs[8:]
    if final:
        fnw_ref = refs[0]
        refs = refs[1:]
    xo_ref, out_ref, bufs_ref, extra_ref, acc_ref, sems, esem = refs
    b = pl.program_id(0)
    k = pl.program_id(1)

    def seg_copy(e, sb, dst, sem):
        be = b * N_EXPERTS + e
        src = pl.multiple_of(off_s[be * (nk + 1) + k] + sb * LANES, BF16_ROWS)
        return pltpu.make_async_copy(xo_ref.at[b, e, pl.ds(src, LANES), :], dst, sem)

    for e in range(N_EXPERTS):
        seg_copy(e, 0, bufs_ref.at[e], sems.at[e]).start()

    bits = pltpu.bitcast(aff_ref[0], I32)
    thr = thr_ref[0]
    gt = bits > thr
    eq = bits == thr
    a = lax.broadcasted_iota(I32, (tt, tt), 0)
    c = lax.broadcasted_iota(I32, (tt, tt), 1)
    lower = jnp.where(c < a, 1.0, 0.0).astype(BF16)
    eqrank = _dot(lower, jnp.where(eq, 1.0, 0.0).astype(BF16)) + eqb_ref[0, 0]
    sel = jnp.where(gt, 1.0, jnp.where(eq, jnp.where(eqrank < need_ref[0].astype(F32), 1.0, 0.0), 0.0))
    rank = jnp.where(sel > 0.0, _dot(lower, sel.astype(BF16)), -1.0)
    wgt = aff_ref[0]
    slot = lax.broadcasted_iota(I32, (tt, LANES), 1).astype(F32)

    acc_ref[...] = jnp.zeros_like(acc_ref)
    for e in range(N_EXPERTS):
        rk = rank[:, e:e + 1]
        ge = wgt[:, e:e + 1]
        seg_copy(e, 0, bufs_ref.at[e], sems.at[e]).wait()
        onehot = jnp.where(rk == slot, 1.0, 0.0).astype(BF16)
        acc_ref[...] += ge * _dot(onehot, bufs_ref[e])
        n = n_s[(b * N_EXPERTS + e) * nk + k]
        for sb in range(1, tt // LANES):
            @pl.when(sb * LANES < n)
            def _():
                cp = seg_copy(e, sb, extra_ref, esem)
                cp.start()
                cp.wait()
                oh = jnp.where(rk == slot + float(sb * LANES), 1.0, 0.0).astype(BF16)
                acc_ref[...] += ge * _dot(oh, extra_ref[...])

    xn = x_ref[0] + gate_ref[0] * acc_ref[...]
    if final:
        xn = xn * lax.rsqrt(jnp.mean(xn * xn, axis=-1, keepdims=True) + EPS) * fnw_ref[...]
    out_ref[0] = xn


def _combine_call(off, n, aff, thr_l, need_l, eqb_l, x, gate, xo, tt, final_w=None):
    bsz, L, d = x.shape
    nk = L // tt
    final = final_w is not None
    in_specs = [pl.BlockSpec((1, tt, LANES), lambda b, k, *_: (b, k, 0)),
                pl.BlockSpec((1, 1, LANES), lambda b, k, *_: (b, 0, 0)),
                pl.BlockSpec((1, 1, LANES), lambda b, k, *_: (b, 0, 0)),
                pl.BlockSpec((1, 1, 1, LANES), lambda b, k, *_: (b, k, 0, 0)),
                pl.BlockSpec((1, tt, d), lambda b, k, *_: (b, k, 0)),
                pl.BlockSpec((1, 1, d), lambda b, k, *_: (b, 0, 0))]
    args = [aff, thr_l, need_l, eqb_l, x, gate]
    if final:
        in_specs.append(pl.BlockSpec((1, d), lambda b, k, *_: (0, 0)))
        args.append(final_w)
    in_specs.append(pl.BlockSpec(memory_space=pl.ANY))
    args.append(xo)
    grid_spec = pltpu.PrefetchScalarGridSpec(
        num_scalar_prefetch=2,
        grid=(bsz, nk),
        in_specs=in_specs,
        out_specs=pl.BlockSpec((1, tt, d), lambda b, k, *_: (b, k, 0)),
        scratch_shapes=[pltpu.VMEM((N_EXPERTS, LANES, d), BF16),
                        pltpu.VMEM((LANES, d), BF16),
                        pltpu.VMEM((tt, d), F32),
                        pltpu.SemaphoreType.DMA((N_EXPERTS,)),
                        pltpu.SemaphoreType.DMA(())])
    return pl.pallas_call(
        functools.partial(_combine_kernel, nk=nk, tt=tt, final=final),
        grid_spec=grid_spec,
        out_shape=jax.ShapeDtypeStruct((bsz, L, d), F32),
        compiler_params=_cparams("parallel", "arbitrary"),
        name="combine",
    )(off, n, *args)


def _moe(x, h2, aff, affT, gate, wg, wu, wd, tt, final_w=None):
    bsz, L, d = x.shape
    ne = N_EXPERTS
    nk = L // tt
    cap = CAPACITY_FACTOR * L // ne
    thr, need, gtc, eqc = _route_call(affT, tt, cap)
    thr = thr[:, :, 0]
    need = need[:, :, 0]
    gtc = gtc[:, :, :nk]
    eqc = eqc[:, :, :nk]
    eqb = jnp.cumsum(eqc, axis=2) - eqc
    n = gtc + jnp.clip(need[:, :, None] - eqb, 0, eqc)
    npad = (n + BF16_ROWS - 1) // BF16_ROWS * BF16_ROWS
    off = jnp.concatenate([jnp.zeros((bsz, ne, 1), I32), jnp.cumsum(npad, axis=2)], axis=2).astype(I32)
    rb = 256
    max_rows = cap + (BF16_ROWS - 1) * nk + LANES
    rows = (max_rows + rb - 1) // rb * rb
    scalars = (thr.reshape(-1), need.reshape(-1), eqb.reshape(-1).astype(I32), off.reshape(-1),
               n.reshape(-1).astype(I32))
    xo = _ffn_call(scalars, affT.reshape(bsz, ne, 1, L), h2, wg, wu, wd, tt, rows, rb)

    def lanes(t, fill):
        return jnp.pad(t, ((0, 0), (0, LANES - ne)), constant_values=fill)[:, None, :]

    eqb_l = jnp.pad(jnp.swapaxes(eqb, 1, 2).astype(F32), ((0, 0), (0, 0), (0, LANES - ne)))[:, :, None, :]
    return _combine_call(scalars[3], scalars[4], aff, lanes(thr, I32_MAX), lanes(need, 0), eqb_l, x, gate, xo,
                         tt, final_w)


def _rope_tables(L, dk):
    rows = L // GRID_W
    r, col = jnp.meshgrid(jnp.arange(rows), jnp.arange(GRID_W), indexing="ij")
    n_freq = dk // 4
    inv = ROPE_BASE ** (-jnp.arange(n_freq, dtype=F32) / n_freq)
    ang = jnp.concatenate([r.reshape(-1, 1).astype(F32) * inv, col.reshape(-1, 1).astype(F32) * inv], axis=-1)
    cos, sin = jnp.cos(ang), jnp.sin(ang)
    return jnp.concatenate([cos, cos], axis=-1), jnp.concatenate([-sin, sin], axis=-1)


def kernel(x, c, ctx, c_ctx, w_mod, b_mod, norm_mix_w, w_in, conv_w, conv_b, ssd_dt_bias, ssd_a_log, ssd_d,
           ssd_norm_w, ret_decay, ret_gn_w, w_ssd_o, w_ret_o, w_o, norm_ffn_w, w_router, w_gate, w_up, w_down,
           final_norm_w):
    bsz, L, d = x.shape
    lc = ctx.shape[1]
    depth = w_mod.shape[0]
    inner = w_ssd_o.shape[1]
    heads = inner // SSD_HEAD_DIM
    vw = w_ret_o.shape[1]
    dv = vw // RET_HEADS
    dk = (P_K - P_Q) // RET_HEADS

    cc = jnp.zeros((8, d), F32).at[:bsz].set(c).at[bsz].set(c_ctx)
    mod_all = _mod_call(cc, w_mod, b_mod).reshape(depth, 8, 6, d)

    rope_tabs = _rope_tables(L, dk)
    hexp = (jnp.arange(LANES, dtype=I32)[:, None] == jnp.arange(inner, dtype=I32)[None, :] // SSD_HEAD_DIM
            ).astype(F32)
    zeros_ssd = jnp.zeros((bsz, SSD_GROUPS * SSD_STATE, inner // SSD_GROUPS), F32)
    zeros_ret = jnp.zeros((bsz, RET_HEADS * dk, dv), F32)

    def lanepad(v):
        return jnp.pad(v, (0, LANES - v.shape[0]))[None, :]

    h_ctx = ctx
    for l in range(depth):
        last = l == depth - 1
        mod_l = mod_all[l, :bsz][:, :, None, :]
        mod_c = jnp.broadcast_to(mod_all[l, bsz][None, :, None, :], (bsz, 6, 1, d))
        wl = w_in[l]
        w_main = jnp.concatenate(
            [wl[:, 0:4096], wl[:, 7200:11296], wl[:, 4096:5120], wl[:, 5152:7200], wl[:, 11296:13344]],
            axis=1).astype(BF16)
        w_dt = jnp.pad(wl[:, 5120:5120 + heads], ((0, 0), (0, LANES - heads))).astype(BF16)
        wso, wro, wo = w_ssd_o[l].astype(BF16), w_ret_o[l].astype(BF16), w_o[l].astype(BF16)
        wr = jnp.pad(w_router[l], ((0, 0), (0, LANES - N_EXPERTS)))
        wg, wu, wd = w_gate[l].astype(BF16), w_up[l].astype(BF16), w_down[l].astype(BF16)
        nmw, nfw = norm_mix_w[l][None, :], norm_ffn_w[l][None, :]
        dskip = jnp.repeat(ssd_d[l], SSD_HEAD_DIM)[None, :]
        lgs = [jnp.broadcast_to(ret_decay[l, dr][:, None], (RET_HEADS, dv)) for dr in range(2)]

        def mixer(tokens, mod, tm, tl, s_ssd, s_ret, rope):
            p, dt = _inproj_call(tokens, nmw, mod[:, 0], mod[:, 1], w_main, w_dt, tm)
            xbc = _conv_call(p, conv_w[l], conv_b[l][None, :], tl)
            ssd_args = lambda dr: (lanepad(ssd_dt_bias[l, dr]), lanepad(ssd_a_log[l, dr]), hexp)
            yf, sf = _ssd_call(xbc, dt, p, *ssd_args(0), s_ssd[0], rev=False)
            ys, sr = _ssd_call(xbc, dt, p, *ssd_args(1), s_ssd[1], rev=True,
                               final_args=(yf, dskip, ssd_norm_w[l][None, :]))
            rf, tf = _ret_call(p, lgs[0], s_ret[0], rev=False, rope_tabs=rope)
            yr, tr = _ret_call(p, lgs[1], s_ret[1], rev=True, rope_tabs=rope,
                               final_args=(rf, ret_gn_w[l][None, :]))
            outs = _mixout_call(ys, yr, p, tokens, mod[:, 2], nfw, mod[:, 3], mod[:, 4], wso, wro, wo, wr, tm)
            return outs, (sf, sr), (tf, tr)

        tmc = min(lc, 512)
        (xc, h2c, affc, affTc), s_ssd, s_ret = mixer(h_ctx, mod_c, tmc, lc, (zeros_ssd, zeros_ssd),
                                                     (zeros_ret, zeros_ret), None)
        (xl, h2l, affl, affTl), _, _ = mixer(x, mod_l, 512, 512, s_ssd, s_ret, rope_tabs)
        x = _moe(xl, h2l, affl, affTl, mod_l[:, 5], wg, wu, wd, min(L, 512),
                 final_w=final_norm_w[None, :] if last else None)
        if not last:
            h_ctx = _moe(xc, h2c, affc, affTc, mod_c[:, 5], wg, wu, wd, min(lc, 512))
    return x
```

```python
import functools
import math

import jax
import jax.numpy as jnp
from jax import lax
from jax.experimental import pallas as pl
from jax.experimental.pallas import tpu as pltpu

F32 = jnp.float32
BF16 = jnp.bfloat16
I32 = jnp.int32
HIGHEST = lax.Precision.HIGHEST

EPS = 1e-6
CHUNK = 128
GRID_W = 64
ROPE_BASE = 10000.0

SSD_HEAD_DIM = 64
SSD_GROUPS = 4
SSD_STATE = 128
SSD_CONV = 3
RET_HEADS = 8
N_EXPERTS = 16
CAPACITY_FACTOR = 2

LANES = 128
BF16_ROWS = 16
VMEM_LIMIT = 56 * 1024 * 1024
NEG_BIG = -1e30
I32_MAX = 2**31 - 1


def _cparams(*sem):
    return pltpu.CompilerParams(dimension_semantics=sem, vmem_limit_bytes=VMEM_LIMIT)


def _sigmoid(x):
    return jax.nn.sigmoid(x)


def _silu(x):
    return x * _sigmoid(x)


def _softplus(x):
    return jnp.maximum(x, 0.0) + jnp.log1p(jnp.exp(-jnp.abs(x)))


def _dot(a, b):
    return jnp.dot(a, b, preferred_element_type=F32)


def _dot_nt(a, b):
    return lax.dot_general(a, b, (((1,), (1,)), ((), ())), preferred_element_type=F32)


def _mod_kernel(c_ref, w_ref, b_ref, o_ref):
    s = _silu(c_ref[...])
    o_ref[0] = jnp.dot(s, w_ref[0], precision=HIGHEST, preferred_element_type=F32) + b_ref[0]


def _mod_call(cc, w_mod, b_mod):
    depth, d, n = w_mod.shape
    tn = 1024
    rows = cc.shape[0]
    return pl.pallas_call(
        _mod_kernel,
        grid=(depth, n // tn),
        in_specs=[pl.BlockSpec((rows, d), lambda l, j: (0, 0)),
                  pl.BlockSpec((1, d, tn), lambda l, j: (l, 0, j)),
                  pl.BlockSpec((1, 1, tn), lambda l, j: (l, 0, j))],
        out_specs=pl.BlockSpec((1, rows, tn), lambda l, j: (l, 0, j)),
        out_shape=jax.ShapeDtypeStruct((depth, rows, n), F32),
        compiler_params=_cparams("parallel", "parallel"),
        name="mod",
    )(cc, w_mod, b_mod.reshape(depth, 1, n))


def _inproj_kernel(x_ref, nw_ref, sh_ref, sc_ref, w_ref, wdt_ref, p_ref, dt_ref, hn_ref):
    @pl.when(pl.program_id(2) == 0)
    def _():
        x = x_ref[0]
        y = x * lax.rsqrt(jnp.mean(x * x, axis=-1, keepdims=True) + EPS) * nw_ref[...]
        hb = (y * (1.0 + sc_ref[0]) + sh_ref[0]).astype(BF16)
        hn_ref[...] = hb
        dt_ref[0] = _dot(hb, wdt_ref[...])

    p_ref[0] = _dot(hn_ref[...], w_ref[...]).astype(BF16)


def _inproj_call(x, nw, shift, scale, w_main, w_dt, tm):
    bsz, L, d = x.shape
    n = w_main.shape[1]
    tn = 1024
    return pl.pallas_call(
        _inproj_kernel,
        grid=(bsz, L // tm, n // tn),
        in_specs=[pl.BlockSpec((1, tm, d), lambda b, i, j: (b, i, 0)),
                  pl.BlockSpec((1, d), lambda b, i, j: (0, 0)),
                  pl.BlockSpec((1, 1, d), lambda b, i, j: (b, 0, 0)),
                  pl.BlockSpec((1, 1, d), lambda b, i, j: (b, 0, 0)),
                  pl.BlockSpec((d, tn), lambda b, i, j: (0, j)),
                  pl.BlockSpec((d, LANES), lambda b, i, j: (0, 0))],
        out_specs=[pl.BlockSpec((1, tm, tn), lambda b, i, j: (b, i, j)),
                   pl.BlockSpec((1, tm, LANES), lambda b, i, j: (b, i, 0))],
        out_shape=[jax.ShapeDtypeStruct((bsz, L, n), BF16),
                   jax.ShapeDtypeStruct((bsz, L, LANES), F32)],
        scratch_shapes=[pltpu.VMEM((tm, d), BF16)],
        compiler_params=_cparams("parallel", "parallel", "arbitrary"),
        name="inproj",
    )(x, nw, shift, scale, w_main, w_dt)


P_Z, P_X, P_V, P_G, P_B, P_C, P_Q, P_K, P_GS, P_GR, P_END = (
    0, 2048, 4096, 6144, 8192, 8704, 9216, 10240, 11264, 12288, 13312)


def _conv_kernel(cur_ref, prev_ref, next_ref, w_ref, b_ref, o_ref, *, n_tiles):
    i = pl.program_id(1)
    x = cur_ref[0].astype(F32)
    tl = x.shape[0]
    prow = jnp.where(i > 0, prev_ref[0][BF16_ROWS - 1:BF16_ROWS, :].astype(F32), 0.0)
    nrow = jnp.where(i < n_tiles - 1, next_ref[0][0:1, :].astype(F32), 0.0)
    rid = lax.broadcasted_iota(I32, x.shape, 0)
    xp = jnp.where(rid == 0, prow, pltpu.roll(x, 1, 0))
    xn = jnp.where(rid == tl - 1, nrow, pltpu.roll(x, tl - 1, 0))
    w = w_ref[...]
    y = xp * w[0:1] + x * w[1:2] + xn * w[2:3] + b_ref[...]
    o_ref[0] = _silu(y).astype(BF16)


def _conv_call(p, conv_w, conv_b, tl):
    bsz, L, _ = p.shape
    tc = 1024
    nc = conv_w.shape[1] // tc
    n_tiles = L // tl
    rpb = tl // BF16_ROWS
    last = L // BF16_ROWS - 1

    def col(j):
        return jnp.where(j < 2, P_X // tc + j, P_B // tc)

    return pl.pallas_call(
        functools.partial(_conv_kernel, n_tiles=n_tiles),
        grid=(bsz, n_tiles, nc),
        in_specs=[pl.BlockSpec((1, tl, tc), lambda b, i, j: (b, i, col(j))),
                  pl.BlockSpec((1, BF16_ROWS, tc), lambda b, i, j: (b, jnp.maximum(i * rpb - 1, 0), col(j))),
                  pl.BlockSpec((1, BF16_ROWS, tc), lambda b, i, j: (b, jnp.minimum((i + 1) * rpb, last), col(j))),
                  pl.BlockSpec((SSD_CONV, tc), lambda b, i, j: (0, j)),
                  pl.BlockSpec((1, tc), lambda b, i, j: (0, j))],
        out_specs=pl.BlockSpec((1, tl, tc), lambda b, i, j: (b, i, j)),
        out_shape=jax.ShapeDtypeStruct((bsz, L, conv_w.shape[1]), BF16),
        compiler_params=_cparams("parallel", "parallel", "parallel"),
        name="conv",
    )(p, p, p, conv_w, conv_b)


def _ssd_kernel(*refs, rev, final):
    if final:
        (x_ref, b_ref, c_ref, dt_ref, dtb_ref, alog_ref, hexp_ref, s0_ref,
         yf_ref, z_ref, dskip_ref, nw_ref, y_ref, sT_ref, st_ref) = refs
    else:
        (x_ref, b_ref, c_ref, dt_ref, dtb_ref, alog_ref, hexp_ref, s0_ref,
         y_ref, sT_ref, st_ref) = refs
    c = pl.program_id(1)
    T = CHUNK
    N = SSD_STATE
    P = SSD_HEAD_DIM
    hg = st_ref.shape[1] // P

    @pl.when(c == 0)
    def _():
        st_ref[...] = s0_ref[0]

    dtv = _softplus(dt_ref[0] + dtb_ref[...])
    la = dtv * (-jnp.exp(alog_ref[...]))
    ii = lax.broadcasted_iota(I32, (T, T), 0)
    jj = lax.broadcasted_iota(I32, (T, T), 1)
    tri = (jj >= ii) if rev else (jj <= ii)
    cs = jnp.dot(tri.astype(F32), la, precision=HIGHEST, preferred_element_type=F32)
    csT = cs.T
    dtT = dtv.T
    last = 0 if rev else T - 1
    wT =dtT * jnp.exp(csT[:, last:last + 1] - csT)
    etot = jnp.exp(cs[last:last + 1, :])
    dexp = jnp.dot(jnp.broadcast_to(etot, (8, LANES)), hexp_ref[...], precision=HIGHEST,
                   preferred_element_type=F32)[0:1]

    xall = x_ref[0]
    ys = []
    for g in range(SSD_GROUPS):
        cg = c_ref[0][:, g * N:(g + 1) * N]
        bg = b_ref[0][:, g * N:(g + 1) * N]
        scores = _dot_nt(cg, bg)
        cgf = cg.astype(F32)
        bgT = bg.astype(F32).T
        yh, kvh = [], []
        for hh in range(hg):
            h = g * hg + hh
            colb = jnp.broadcast_to(cs[:, h:h + 1], (T, T))
            row = csT[h:h + 1, :]
            decay = jnp.exp(jnp.where(tri, colb - row, NEG_BIG))
            m = (scores * decay * dtT[h:h + 1, :]).astype(BF16)
            ce = (cgf * jnp.exp(colb)).astype(BF16)
            xh = xall[:, h * P:(h + 1) * P]
            sh = st_ref[g * N:(g + 1) * N, hh * P:(hh + 1) * P].astype(BF16)
            yh.append(_dot(jnp.concatenate([m, ce], axis=1), jnp.concatenate([xh, sh], axis=0)))
            kvh.append(_dot((bgT * wT[h:h + 1, :]).astype(BF16), xh))
        ys.append(jnp.concatenate(yh, axis=1))
        kv = jnp.concatenate(kvh, axis=1)
        w = hg * P
        st_ref[g * N:(g + 1) * N, :] = st_ref[g * N:(g + 1) * N, :] * dexp[:, g * w:(g + 1) * w] + kv

    y = jnp.concatenate(ys, axis=1)
    if final:
        y = dskip_ref[...] * xall.astype(F32) + yf_ref[0] + y
        yg = y * _silu(z_ref[0].astype(F32))
        w = hg * P
        outs = []
        for g in range(SSD_GROUPS):
            t = yg[:, g * w:(g + 1) * w]
            outs.append(t * lax.rsqrt(jnp.mean(t * t, axis=-1, keepdims=True) + EPS))
        y_ref[0] = (jnp.concatenate(outs, axis=1) * nw_ref[...]).astype(y_ref.dtype)
    else:
        y_ref[0] = y

    @pl.when(c == pl.num_programs(1) - 1)
    def _():
        sT_ref[0] = st_ref[...]


def _ssd_call(xbc, dt, p, dtb, alog, hexp, s0, rev, final_args=None):
    bsz, L, _ = xbc.shape
    nch = L // CHUNK
    inner = hexp.shape[1]
    srows = SSD_GROUPS * SSD_STATE
    scols = inner // SSD_GROUPS
    final = final_args is not None

    def tok(b, c):
        return (b, nch - 1 - c, 0) if rev else (b, c, 0)

    def tokcol(blk):
        return lambda b, c: tok(b, c)[:2] + (blk,)

    gn = SSD_GROUPS * SSD_STATE
    in_specs = [pl.BlockSpec((1, CHUNK, inner), tokcol(0)),
                pl.BlockSpec((1, CHUNK, gn), tokcol(inner // gn)),
                pl.BlockSpec((1, CHUNK, gn), tokcol(inner // gn + 1)),
                pl.BlockSpec((1, CHUNK, LANES), tok),
                pl.BlockSpec((1, LANES), lambda b, c: (0, 0)),
                pl.BlockSpec((1, LANES), lambda b, c: (0, 0)),
                pl.BlockSpec((LANES, inner), lambda b, c: (0, 0)),
                pl.BlockSpec((1, srows, scols), lambda b, c: (b, 0, 0))]
    args = [xbc, xbc, xbc, dt, dtb, alog, hexp, s0]
    if final:
        yf, dskip, nw = final_args
        in_specs += [pl.BlockSpec((1, CHUNK, inner), tok),
                     pl.BlockSpec((1, CHUNK, inner), tokcol(P_Z // inner)),
                     pl.BlockSpec((1, inner), lambda b, c: (0, 0)),
                     pl.BlockSpec((1, inner), lambda b, c: (0, 0))]
        args += [yf, p, dskip, nw]
    return pl.pallas_call(
        functools.partial(_ssd_kernel, rev=rev, final=final),
        grid=(bsz, nch),
        in_specs=in_specs,
        out_specs=[pl.BlockSpec((1, CHUNK, inner), tok),
                   pl.BlockSpec((1, srows, scols), lambda b, c: (b, 0, 0))],
        out_shape=[jax.ShapeDtypeStruct((bsz, L, inner), BF16 if final else F32),
                   jax.ShapeDtypeStruct((bsz, srows, scols), F32)],
        scratch_shapes=[pltpu.VMEM((srows, scols), F32)],
        compiler_params=_cparams("parallel", "arbitrary"),
        name="ssd_rev" if rev else "ssd_fwd",
    )(*args)


def _ret_kernel(*refs, rev, final, rope):
    refs = list(refs)
    q_ref, k_ref, v_ref, lg_ref = refs[:4]
    refs = refs[4:]
    if rope:
        cos_ref, sin_ref = refs[:2]
        refs = refs[2:]
    s0_ref = refs[0]
    refs = refs[1:]
    if final:
        yf_ref, g_ref, gnw_ref = refs[:3]
        refs = refs[3:]
    y_ref, sT_ref, st_ref, dec_ref, ein_ref, wout_ref, etot_ref = refs
    c = pl.program_id(1)
    T = CHUNK
    dk = q_ref.shape[2] // RET_HEADS
    dv = v_ref.shape[2] // RET_HEADS

    @pl.when(c == 0)
    def _():
        st_ref[...] = s0_ref[0]
        ii = lax.broadcasted_iota(I32, (T, T), 0)
        jj = lax.broadcasted_iota(I32, (T, T), 1)
        tri = (jj >= ii) if rev else (jj <= ii)
        dist = ((jj - ii) if rev else (ii - jj)).astype(F32)
        ri = lax.broadcasted_iota(I32, (T, dk), 0).astype(F32)
        n_in = (T - ri) if rev else (ri + 1.0)
        n_out = ri if rev else (T - 1.0 - ri)
        for h in range(RET_HEADS):
            la2 = -_softplus(-lg_ref[h:h + 1, :])
            la = la2[:, :dk]
            dec_ref[h] = jnp.exp(jnp.where(tri, dist * la, NEG_BIG))
            ein_ref[h] = jnp.exp(n_in * la)
            wout_ref[h] = jnp.exp(n_out * la)
            etot_ref[h] = jnp.broadcast_to(jnp.exp(float(T) * la2), (8, dv))

    kscale = dk ** -0.5
    qa = q_ref[0]
    ka = k_ref[0]
    va = v_ref[0]
    ys = []
    for h in range(RET_HEADS):
        qf = qa[:, h * dk:(h + 1) * dk].astype(F32)
        kf = ka[:, h * dk:(h + 1) * dk].astype(F32)
        if rope:
            qf = qf * cos_ref[...] + pltpu.roll(qf, dk // 2, 1) * sin_ref[...]
            kf = kf * cos_ref[...] + pltpu.roll(kf, dk // 2, 1) * sin_ref[...]
        kf = kf * kscale
        scores = _dot_nt(qf.astype(BF16), kf.astype(BF16))
        m = (scores * dec_ref[h]).astype(BF16)
        qe = (qf * ein_ref[h]).astype(BF16)
        vh = va[:, h * dv:(h + 1) * dv]
        sh = st_ref[h * dk:(h + 1) * dk, :]
        ys.append(_dot(jnp.concatenate([m, qe], axis=1), jnp.concatenate([vh, sh.astype(BF16)], axis=0)))
        kw = (kf * wout_ref[h]).T.astype(BF16)
        st_ref[h * dk:(h + 1) * dk, :] = sh * etot_ref[h][0:1] + _dot(kw, vh)

    if final:
        outs = []
        for h in range(RET_HEADS):
            t = ys[h] + yf_ref[0][:, h * dv:(h + 1) * dv]
            t = t - jnp.mean(t, axis=-1, keepdims=True)
            outs.append(t * lax.rsqrt(jnp.mean(t * t, axis=-1, keepdims=True) + EPS))
        y = jnp.concatenate(outs, axis=1) * gnw_ref[...]
        y_ref[0] = (_silu(g_ref[0].astype(F32)) * y).astype(y_ref.dtype)
    else:
        y_ref[0] = jnp.concatenate(ys, axis=1)

    @pl.when(c == pl.num_programs(1) - 1)
    def _():
        sT_ref[0] = st_ref[...]


def _ret_call(p, lg, s0, rev, rope_tabs=None, final_args=None):
    bsz, L, _ = p.shape
    nch = L // CHUNK
    qk = P_K - P_Q
    vw = P_G - P_V
    dk = qk // RET_HEADS
    dv = vw // RET_HEADS
    final = final_args is not None
    rope = rope_tabs is not None

    def tok(b, c):
        return (b, nch - 1 - c, 0) if rev else (b, c, 0)

    def tokcol(blk):
        return lambda b, c: tok(b, c)[:2] + (blk,)

    def chunk_only(b, c):
        return tok(b, c)[1:]

    in_specs = [pl.BlockSpec((1, CHUNK, qk), tokcol(P_Q // qk)),
                pl.BlockSpec((1, CHUNK, qk), tokcol(P_K // qk)),
                pl.BlockSpec((1, CHUNK, vw), tokcol(P_V // vw)),
                pl.BlockSpec((RET_HEADS, dv), lambda b, c: (0, 0))]
    args = [p, p, p, lg]
    if rope:
        in_specs += [pl.BlockSpec((CHUNK, dk), chunk_only), pl.BlockSpec((CHUNK, dk), chunk_only)]
        args += list(rope_tabs)
    in_specs += [pl.BlockSpec((1, RET_HEADS * dk, dv), lambda b, c: (b, 0, 0))]
    args += [s0]
    if final:
        yf, gnw = final_args
        in_specs += [pl.BlockSpec((1, CHUNK, vw), tok),
                     pl.BlockSpec((1, CHUNK, vw), tokcol(P_G // vw)),
                     pl.BlockSpec((1, vw), lambda b, c: (0, 0))]
        args += [yf, p, gnw]
    return pl.pallas_call(
        functools.partial(_ret_kernel, rev=rev, final=final, rope=rope),
        grid=(bsz, nch),
        in_specs=in_specs,
        out_specs=[pl.BlockSpec((1, CHUNK, vw), tok),
                   pl.BlockSpec((1, RET_HEADS * dk, dv), lambda b, c: (b, 0, 0))],
        out_shape=[jax.ShapeDtypeStruct((bsz, L, vw), BF16 if final else F32),
                   jax.ShapeDtypeStruct((bsz, RET_HEADS * dk, dv), F32)],
        scratch_shapes=[pltpu.VMEM((RET_HEADS * dk, dv), F32),
                        pltpu.VMEM((RET_HEADS, CHUNK, CHUNK), F32),
                        pltpu.VMEM((RET_HEADS, CHUNK, dk), F32),
                        pltpu.VMEM((RET_HEADS, CHUNK, dk), F32),
                        pltpu.VMEM((RET_HEADS, 8, dv), F32)],
        compiler_params=_cparams("parallel", "arbitrary"),
        name="ret_rev" if rev else "ret_fwd",
    )(*args)


def _mixout_kernel(ys_ref, yr_ref, gs_ref, gr_ref, x_ref, gate_ref, nw_ref, sh_ref, sc_ref,
                   wso_ref, wro_ref, wo_ref, wr_ref, xn_ref, h2_ref, aff_ref, affT_ref):
    a = _dot(ys_ref[0], wso_ref[...])
    b = _dot(yr_ref[0], wro_ref[...])
    m = _sigmoid(gs_ref[0].astype(F32)) * a + _sigmoid(gr_ref[0].astype(F32)) * b
    xn = x_ref[0] + gate_ref[0] * _dot(m.astype(BF16), wo_ref[...])
    xn_ref[0] = xn
    y = xn * lax.rsqrt(jnp.mean(xn * xn, axis=-1, keepdims=True) + EPS) * nw_ref[...]
    h2 = y * (1.0 + sc_ref[0]) + sh_ref[0]
    h2_ref[0] = h2.astype(BF16)
    logits = jnp.dot(h2, wr_ref[...], precision=HIGHEST, preferred_element_type=F32)
    lane = lax.broadcasted_iota(I32, logits.shape, 1)
    logits = jnp.where(lane < N_EXPERTS, logits, NEG_BIG)
    e = jnp.exp(logits - jnp.max(logits, axis=-1, keepdims=True))
    aff = e / jnp.sum(e, axis=-1, keepdims=True)
    aff_ref[0] = aff
    affT_ref[0] = aff.T[:N_EXPERTS, :]


def _mixout_call(ys, yr, p, x, gate, nw, shift, scale, wso, wro, wo, wr, tm):
    bsz, L, d = x.shape
    inner = ys.shape[2]
    vw = yr.shape[2]
    const = lambda b, i: (0, 0)
    vec = lambda b, i: (b, 0, 0)
    return pl.pallas_call(
        _mixout_kernel,
        grid=(bsz, L // tm),
        in_specs=[pl.BlockSpec((1, tm, inner), lambda b, i: (b, i, 0)),
                  pl.BlockSpec((1, tm, vw), lambda b, i: (b, i, 0)),
                  pl.BlockSpec((1, tm, d), lambda b, i: (b, i, P_GS // d)),
                  pl.BlockSpec((1, tm, d), lambda b, i: (b, i, P_GR // d)),
                  pl.BlockSpec((1, tm, d), lambda b, i: (b, i, 0)),
                  pl.BlockSpec((1, 1, d), vec),
                  pl.BlockSpec((1, d), const),
                  pl.BlockSpec((1, 1, d), vec),
                  pl.BlockSpec((1, 1, d), vec),
                  pl.BlockSpec((inner, d), const, pipeline_mode=pl.Buffered(1)),
                  pl.BlockSpec((vw, d), const, pipeline_mode=pl.Buffered(1)),
                  pl.BlockSpec((d, d), const, pipeline_mode=pl.Buffered(1)),
                  pl.BlockSpec((d, LANES), const, pipeline_mode=pl.Buffered(1))],
        out_specs=[pl.BlockSpec((1, tm, d), lambda b, i: (b, i, 0)),
                   pl.BlockSpec((1, tm, d), lambda b, i: (b, i, 0)),
                   pl.BlockSpec((1, tm, LANES), lambda b, i: (b, i, 0)),
                   pl.BlockSpec((1, N_EXPERTS, tm), lambda b, i: (b, 0, i))],
        out_shape=[jax.ShapeDtypeStruct((bsz, L, d), F32),
                   jax.ShapeDtypeStruct((bsz, L, d), BF16),
                   jax.ShapeDtypeStruct((bsz, L, LANES), F32),
                   jax.ShapeDtypeStruct((bsz, N_EXPERTS, L), F32)],
        compiler_params=_cparams("parallel", "parallel"),
        name="mixout",
    )(ys, yr, p, p, x, gate, nw, shift, scale, wso, wro, wo, wr)


def _route_kernel(affT_ref, upper_ref, slot_ref, slotT_ref, cnt_ref, *, cap, tt):
    bits = pltpu.bitcast(affT_ref[0], I32)
    L = bits.shape[1]

    def body(i, t):
        cand = t | lax.shift_left(jnp.int32(1), 30 - i)
        cnt = jnp.sum(jnp.where(bits >= cand, 1.0, 0.0), axis=1, keepdims=True)
        return jnp.where(cnt >= cap, cand, t)

    thr = lax.fori_loop(0, 31, body, jnp.zeros((N_EXPERTS, 1), I32))
    need = cap - jnp.sum(jnp.where(bits > thr, 1.0, 0.0), axis=1, keepdims=True)
    lane = lax.broadcasted_iota(I32, (N_EXPERTS, LANES), 1)
    eq_before = jnp.zeros((N_EXPERTS, 1), F32)
    cnt = jnp.zeros((N_EXPERTS, LANES), F32)
    upper = upper_ref[...]
    for k in range(L // tt):
        bk = bits[:, k * tt:(k + 1) * tt]
        eq = jnp.where(bk == thr, 1.0, 0.0)
        eqrank = _dot(eq.astype(BF16), upper) + eq_before
        sel = jnp.where(bk > thr, 1.0, jnp.where(eqrank < need, eq, 0.0))
        code = jnp.where(sel > 0.0, _dot(sel.astype(BF16), upper), -1.0)
        slot_ref[0, :, k * tt:(k + 1) * tt] = code
        padded = jnp.concatenate([code, jnp.full((LANES - N_EXPERTS, tt), -1.0, F32)], axis=0)
        slotT_ref[0, k * tt:(k + 1) * tt, :] = padded.T
        cnt = jnp.where(lane == k, jnp.sum(sel, axis=1, keepdims=True), cnt)
        eq_before = eq_before + jnp.sum(eq, axis=1, keepdims=True)
    cnt_ref[0] = cnt.astype(I32)


def _route_call(affT, tt, cap):
    bsz, ne, L = affT.shape
    upper = (jnp.arange(tt, dtype=I32)[:, None] < jnp.arange(tt, dtype=I32)[None, :]).astype(BF16)
    return pl.pallas_call(
        functools.partial(_route_kernel, cap=cap, tt=tt),
        grid=(bsz,),
        in_specs=[pl.BlockSpec((1, ne, L), lambda b: (b, 0, 0)),
                  pl.BlockSpec((tt, tt), lambda b: (0, 0))],
        out_specs=[pl.BlockSpec((1, ne, L), lambda b: (b, 0, 0)),
                   pl.BlockSpec((1, L, LANES), lambda b: (b, 0, 0)),
                   pl.BlockSpec((1, ne, LANES), lambda b: (b, 0, 0))],
        out_shape=[jax.ShapeDtypeStruct((bsz, ne, L), F32),
                   jax.ShapeDtypeStruct((bsz, L, LANES), F32),
                   jax.ShapeDtypeStruct((bsz, ne, LANES), I32)],
        compiler_params=_cparams("parallel"),
        name="route",
    )(affT, upper)


def _ffn_kernel(off_s, n_s, slot_ref, h_ref, wg_ref, wu_ref, wd_ref, o_ref, xin_ref, *, nk, tt, rb):
    e = pl.program_id(0)
    b = pl.program_id(1)
    k = pl.program_id(2)
    be = b * N_EXPERTS + e
    rows = xin_ref.shape[0]

    @pl.when(k == 0)
    def _():
        xin_ref[...] = jnp.zeros_like(xin_ref)

    code = slot_ref[0, 0]
    n = n_s[be * nk + k]
    off = off_s[be * (nk + 1) + k]
    hb = h_ref[0]
    slot = lax.broadcasted_iota(I32, (LANES, tt), 0).astype(F32)
    for sb in range(tt // LANES):
        @pl.when(sb * LANES < n)
        def _():
            onehot = jnp.where(code == slot + float(sb * LANES), 1.0, 0.0).astype(BF16)
            dst = pl.multiple_of(off + sb * LANES, BF16_ROWS)
            xin_ref[pl.ds(dst, LANES), :] = _dot(onehot, hb).astype(BF16)

    @pl.when(k == nk - 1)
    def _():
        used = off_s[be * (nk + 1) + nk] + LANES
        for r in range(rows // rb):
            @pl.when(r * rb < used)
            def _():
                xb = xin_ref[r * rb:(r + 1) * rb, :]
                hid = (_silu(_dot(xb, wg_ref[0])) * _dot(xb, wu_ref[0])).astype(BF16)
                o_ref[0, 0, r * rb:(r + 1) * rb, :] = _dot(hid, wd_ref[0]).astype(BF16)

            @pl.when(r * rb >= used)
            def _():
                o_ref[0, 0, r * rb:(r + 1) * rb, :] = jnp.zeros((rb, o_ref.shape[3]), BF16)


def _ffn_call(off, n, slot4, h2, wg, wu, wd, tt, rows, rb):
    bsz, L, d = h2.shape
    ne, _, ff = wg.shape
    nk = L // tt
    grid_spec = pltpu.PrefetchScalarGridSpec(
        num_scalar_prefetch=2,
        grid=(ne, bsz, nk),
        in_specs=[pl.BlockSpec((1, 1, 1, tt), lambda e, b, k, *_: (b, e, 0, k)),
                  pl.BlockSpec((1, tt, d), lambda e, b, k, *_: (b, k, 0)),
                  pl.BlockSpec((1, d, ff), lambda e, b, k, *_: (e, 0, 0)),
                  pl.BlockSpec((1, d, ff), lambda e, b, k, *_: (e, 0, 0)),
                  pl.BlockSpec((1, ff, d), lambda e, b, k, *_: (e, 0, 0))],
        out_specs=pl.BlockSpec((1, 1, rows, d), lambda e, b, k, *_: (b, e, 0, 0)),
        scratch_shapes=[pltpu.VMEM((rows, d), BF16)])
    return pl.pallas_call(
        functools.partial(_ffn_kernel, nk=nk, tt=tt, rb=rb),
        grid_spec=grid_spec,
        out_shape=jax.ShapeDtypeStruct((bsz, ne, rows, d), BF16),
        compiler_params=_cparams("parallel", "arbitrary", "arbitrary"),
        name="ffn",
    )(off, n, slot4, h2, wg, wu, wd)


def _combine_kernel(*refs, nk, tt, final):
    off_s, n_s, slotT_ref, aff_ref, x_ref, gate_ref = refs[:6]
    refs = refs[6:]
    if final:
        fnw_ref = refs[0]
        refs = refs[1:]
    xo_ref, out_ref, bufs_ref, extra_ref, acc_ref, sems, esem = refs
    b = pl.program_id(0)
    k = pl.program_id(1)

    def seg_copy(e, sb, dst, sem):
        be = b * N_EXPERTS + e
        src = pl.multiple_of(off_s[be * (nk + 1) + k] + sb * LANES, BF16_ROWS)
        return pltpu.make_async_copy(xo_ref.at[b, e, pl.ds(src, LANES), :], dst, sem)

    def first_copy(e):
        return seg_copy(e, 0, bufs_ref.at[pl.ds(e * LANES, LANES), :], sems.at[e])

    for e in range(N_EXPERTS):
        first_copy(e).start()

    code = slotT_ref[0]
    wgt = aff_ref[0]
    slot = lax.broadcasted_iota(I32, (tt, LANES), 1).astype(F32)

    def split(w):
        hi = w.astype(BF16)
        return hi, (w - hi.astype(F32)).astype(BF16)

    his, los = [], []
    for e in range(N_EXPERTS):
        hi, lo = split(jnp.where(code[:, e:e + 1] == slot, wgt[:, e:e + 1], 0.0))
        his.append(hi)
        los.append(lo)
    for e in range(N_EXPERTS):
        first_copy(e).wait()
    rows = bufs_ref[...]
    acc_ref[...] = _dot(jnp.concatenate(his, axis=1), rows) + _dot(jnp.concatenate(los, axis=1), rows)

    for e in range(N_EXPERTS):
        n = n_s[(b * N_EXPERTS + e) * nk + k]
        for sb in range(1, tt // LANES):
            @pl.when(sb * LANES < n)
            def _():
                cp = seg_copy(e, sb, extra_ref, esem)
                cp.start()
                cp.wait()
                hi, lo = split(jnp.where(code[:, e:e + 1] == slot + float(sb * LANES), wgt[:, e:e + 1], 0.0))
                acc_ref[...] += _dot(hi, extra_ref[...]) + _dot(lo, extra_ref[...])

    xn = x_ref[0] + gate_ref[0] * acc_ref[...]
    if final:
        xn = xn * lax.rsqrt(jnp.mean(xn * xn, axis=-1, keepdims=True) + EPS) * fnw_ref[...]
    out_ref[0] = xn


def _combine_call(off, n, slotT, aff, x, gate, xo, tt, final_w=None):
    bsz, L, d = x.shape
    nk = L // tt
    final = final_w is not None
    in_specs = [pl.BlockSpec((1, tt, LANES), lambda b, k, *_: (b, k, 0)),
                pl.BlockSpec((1, tt, LANES), lambda b, k, *_: (b, k, 0)),
                pl.BlockSpec((1, tt, d), lambda b, k, *_: (b, k, 0)),
                pl.BlockSpec((1, 1, d), lambda b, k, *_: (b, 0, 0))]
    args = [slotT, aff, x, gate]
    if final:
        in_specs.append(pl.BlockSpec((1, d), lambda b, k, *_: (0, 0)))
        args.append(final_w)
    in_specs.append(pl.BlockSpec(memory_space=pl.ANY))
    args.append(xo)
    grid_spec = pltpu.PrefetchScalarGridSpec(
        num_scalar_prefetch=2,
        grid=(bsz, nk),
        in_specs=in_specs,
        out_specs=pl.BlockSpec((1, tt, d), lambda b, k, *_: (b, k, 0)),
        scratch_shapes=[pltpu.VMEM((N_EXPERTS * LANES, d), BF16),
                        pltpu.VMEM((LANES, d), BF16),
                        pltpu.VMEM((tt, d), F32),
                        pltpu.SemaphoreType.DMA((N_EXPERTS,)),
                        pltpu.SemaphoreType.DMA(())])
    return pl.pallas_call(
        functools.partial(_combine_kernel, nk=nk, tt=tt, final=final),
        grid_spec=grid_spec,
        out_shape=jax.ShapeDtypeStruct((bsz, L, d), F32),
        compiler_params=_cparams("parallel", "arbitrary"),
        name="combine",
    )(off, n, *args)


def _moe(x, h2, aff, affT, gate, wg, wu, wd, tt, final_w=None):
    bsz, L, d = x.shape
    ne = N_EXPERTS
    nk = L // tt
    cap = CAPACITY_FACTOR * L // ne
    slot, slotT, cnt = _route_call(affT, tt, cap)
    n = cnt[:, :, :nk]
    npad = (n + BF16_ROWS - 1) // BF16_ROWS * BF16_ROWS
    off = jnp.concatenate([jnp.zeros((bsz, ne, 1), I32), jnp.cumsum(npad, axis=2)], axis=2).astype(I32)
    off, n = off.reshape(-1), n.reshape(-1)
    rb = 256
    max_rows = cap + (BF16_ROWS - 1) * nk + LANES
    rows = (max_rows + rb - 1) // rb * rb
    xo = _ffn_call(off, n, slot.reshape(bsz, ne, 1, L), h2, wg, wu, wd, tt, rows, rb)
    return _combine_call(off, n, slotT, aff, x, gate, xo, tt, final_w)


def _rope_tables(L, dk):
    rows = L // GRID_W
    r, col = jnp.meshgrid(jnp.arange(rows), jnp.arange(GRID_W), indexing="ij")
    n_freq = dk // 4
    inv = ROPE_BASE ** (-jnp.arange(n_freq, dtype=F32) / n_freq)
    ang = jnp.concatenate([r.reshape(-1, 1).astype(F32) * inv, col.reshape(-1, 1).astype(F32) * inv], axis=-1)
    cos, sin = jnp.cos(ang), jnp.sin(ang)
    return jnp.concatenate([cos, cos], axis=-1), jnp.concatenate([-sin, sin], axis=-1)


def kernel(x, c, ctx, c_ctx, w_mod, b_mod, norm_mix_w, w_in, conv_w, conv_b, ssd_dt_bias, ssd_a_log, ssd_d,
           ssd_norm_w, ret_decay, ret_gn_w, w_ssd_o, w_ret_o, w_o, norm_ffn_w, w_router, w_gate, w_up, w_down,
           final_norm_w):
    bsz, L, d = x.shape
    lc = ctx.shape[1]
    depth = w_mod.shape[0]
    inner = w_ssd_o.shape[1]
    heads = inner // SSD_HEAD_DIM
    vw = w_ret_o.shape[1]
    dv = vw // RET_HEADS
    dk = (P_K - P_Q) // RET_HEADS

    cc = jnp.zeros((8, d), F32).at[:bsz].set(c).at[bsz].set(c_ctx)
    mod_all = _mod_call(cc, w_mod, b_mod).reshape(depth, 8, 6, d)

    rope_tabs = _rope_tables(L, dk)
    hexp = (jnp.arange(LANES, dtype=I32)[:, None] == jnp.arange(inner, dtype=I32)[None, :] // SSD_HEAD_DIM
            ).astype(F32)
    zeros_ssd = jnp.zeros((bsz, SSD_GROUPS * SSD_STATE, inner // SSD_GROUPS), F32)
    zeros_ret = jnp.zeros((bsz, RET_HEADS * dk, dv), F32)

    def lanepad(v):
        return jnp.pad(v, (0, LANES - v.shape[0]))[None, :]

    h_ctx = ctx
    for l in range(depth):
        last = l == depth - 1
        mod_l = mod_all[l, :bsz][:, :, None, :]
        mod_c = jnp.broadcast_to(mod_all[l, bsz][None, :, None, :], (bsz, 6, 1, d))
        wl = w_in[l]
        w_main = jnp.concatenate(
            [wl[:, 0:4096], wl[:, 7200:11296], wl[:, 4096:5120], wl[:, 5152:7200], wl[:, 11296:13344]],
            axis=1).astype(BF16)
        w_dt = jnp.pad(wl[:, 5120:5120 + heads], ((0, 0), (0, LANES - heads))).astype(BF16)
        wso, wro, wo = w_ssd_o[l].astype(BF16), w_ret_o[l].astype(BF16), w_o[l].astype(BF16)
        wr = jnp.pad(w_router[l], ((0, 0), (0, LANES - N_EXPERTS)))
        wg, wu, wd = w_gate[l].astype(BF16), w_up[l].astype(BF16), w_down[l].astype(BF16)
        nmw, nfw = norm_mix_w[l][None, :], norm_ffn_w[l][None, :]
        dskip = jnp.repeat(ssd_d[l], SSD_HEAD_DIM)[None, :]
        lgs = [jnp.broadcast_to(ret_decay[l, dr][:, None], (RET_HEADS, dv)) for dr in range(2)]

        def mixer(tokens, mod, tm_in, tm, tl, s_ssd, s_ret, rope, project=True):
            p, dt = _inproj_call(tokens, nmw, mod[:, 0], mod[:, 1], w_main, w_dt, tm_in)
            xbc = _conv_call(p, conv_w[l], conv_b[l][None, :], tl)
            ssd_args = lambda dr: (lanepad(ssd_dt_bias[l, dr]), lanepad(ssd_a_log[l, dr]), hexp)
            yf, sf = _ssd_call(xbc, dt, p, *ssd_args(0), s_ssd[0], rev=False)
            ys, sr = _ssd_call(xbc, dt, p, *ssd_args(1), s_ssd[1], rev=True,
                               final_args=(yf, dskip, ssd_norm_w[l][None, :]))
            rf, tf = _ret_call(p, lgs[0], s_ret[0], rev=False, rope_tabs=rope)
            yr, tr = _ret_call(p, lgs[1], s_ret[1], rev=True, rope_tabs=rope,
                               final_args=(rf, ret_gn_w[l][None, :]))
            if not project:
                return None, (sf, sr), (tf, tr)
            outs = _mixout_call(ys, yr, p, tokens, mod[:, 2], nfw, mod[:, 3], mod[:, 4], wso, wro, wo, wr, tm)
            return outs, (sf, sr), (tf, tr)

        tmc = min(lc, 512)
        ctx_out, s_ssd, s_ret = mixer(h_ctx, mod_c, tmc, tmc, lc, (zeros_ssd, zeros_ssd),
                                      (zeros_ret, zeros_ret), None, project=not last)
        (xl, h2l, affl, affTl), _, _ = mixer(x, mod_l, min(L, 1024), 512, 512, s_ssd, s_ret, rope_tabs)
        x = _moe(xl, h2l, affl, affTl, mod_l[:, 5], wg, wu, wd, min(L, 512),
                 final_w=final_norm_w[None, :] if last else None)
        if not last:
            xc, h2c, affc, affTc = ctx_out
            h_ctx = _moe(xc, h2c, affc, affTc, mod_c[:, 5], wg, wu, wd, min(lc, 512))
    return x
```
